```python
import math
import jax
import jax.numpy as jnp
from jax import lax
import numpy as np

D_MODEL = 1024
BATCH = 8
SEQ = 2048
DEPTH = 4

D_MIX = 2 * D_MODEL
SSD_WIDTH = D_MIX // 2
ATTN_WIDTH = D_MIX // 4
CM_CHANNELS = D_MIX // 4
SSD_HEAD_DIM = 64
SSD_HEADS = SSD_WIDTH // SSD_HEAD_DIM
SSD_STATE = 128
SSD_GROUPS = 2
SSD_CONV = 4
SSD_CHUNK = 128
SSD_XBC = SSD_WIDTH + 2 * SSD_GROUPS * SSD_STATE
ATTN_HEAD_DIM = 64
ATTN_Q_HEADS = ATTN_WIDTH // ATTN_HEAD_DIM
ATTN_KV_HEADS = 2
WINDOW = 128
ATTN_BLOCK = WINDOW
ROPE_THETA = 10000.0
CM_CONV_WIDTH = 31
D_FF = 4 * D_MODEL
RMS_EPS = 1e-6
LN_EPS = 1e-5
IN_SIZES = (SSD_WIDTH, SSD_XBC, SSD_HEADS,
            ATTN_Q_HEADS * ATTN_HEAD_DIM, ATTN_KV_HEADS * ATTN_HEAD_DIM, ATTN_KV_HEADS * ATTN_HEAD_DIM,
            2 * CM_CHANNELS)
N_IN = sum(IN_SIZES)

kernel_name = "hybrid_ssd_swa_conformer_parallel_heads"


def _split(u, sizes):
    idx, acc = [], 0
    for s in sizes[:-1]:
        acc += s
        idx.append(acc)
    return jnp.split(u, idx, axis=-1)


def rms_norm(x, w, eps=RMS_EPS):
    xf = x.astype(jnp.float32)
    y = xf * lax.rsqrt(jnp.mean(xf * xf, axis=-1, keepdims=True) + eps)
    return (y * w.astype(jnp.float32)).astype(x.dtype)


def layer_norm(x, w, b, eps=LN_EPS):
    xf = x.astype(jnp.float32)
    mu = jnp.mean(xf, axis=-1, keepdims=True)
    var = jnp.mean(jnp.square(xf - mu), axis=-1, keepdims=True)
    y = (xf - mu) * lax.rsqrt(var + eps)
    return (y * w.astype(jnp.float32) + b.astype(jnp.float32)).astype(x.dtype)


def gated_rms_norm(y, z, w):
    g = y.astype(jnp.float32) * jax.nn.silu(z.astype(jnp.float32))
    shp = g.shape
    g = g.reshape(shp[:-1] + (SSD_GROUPS, shp[-1] // SSD_GROUPS))
    g = g * lax.rsqrt(jnp.mean(g * g, axis=-1, keepdims=True) + RMS_EPS)
    return (g.reshape(shp) * w.astype(jnp.float32)).astype(y.dtype)


def causal_depthwise_conv(u, w, b):
    k_w, ch = w.shape
    out = lax.conv_general_dilated(
        u, w[:, None, :].astype(u.dtype), window_strides=(1,), padding=[(k_w - 1, 0)],
        dimension_numbers=("NWC", "WIO", "NWC"), feature_group_count=ch)
    return out + b.astype(u.dtype)


def rope_tables(seq_len, dim):
    inv_freq = ROPE_THETA ** (-jnp.arange(0, dim, 2, dtype=jnp.float32) / dim)
    ang = jnp.arange(seq_len, dtype=jnp.float32)[:, None] * inv_freq[None, :]
    return jnp.cos(ang), jnp.sin(ang)


def apply_rope(x, cos, sin):
    xf = x.astype(jnp.float32)
    x1, x2 = jnp.split(xf, 2, axis=-1)
    c = cos[None, :, None, :]
    s = sin[None, :, None, :]
    return jnp.concatenate([x1 * c - x2 * s, x2 * c + x1 * s], axis=-1).astype(x.dtype)


def ssd_chunked(x, dt, a, bm, cm, d_skip):
    bsz, seq, nh, hp = x.shape
    q = SSD_CHUNK
    nc = seq // q
    r = nh // SSD_GROUPS
    xf = x.astype(jnp.float32).reshape(bsz, nc, q, SSD_GROUPS, r, hp)
    dtc = dt.reshape(bsz, nc, q, SSD_GROUPS, r)
    bf = bm.astype(jnp.float32).reshape(bsz, nc, q, SSD_GROUPS, SSD_STATE)
    cf = cm.astype(jnp.float32).reshape(bsz, nc, q, SSD_GROUPS, SSD_STATE)
    a_dt = jnp.moveaxis(dtc * a.reshape(SSD_GROUPS, r), 2, -1)
    a_cs = jnp.cumsum(a_dt, axis=-1)
    xdt = xf * dtc[..., None]
    causal = jnp.tril(jnp.ones((q, q), dtype=bool))
    seg = a_cs[..., :, None] - a_cs[..., None, :]
    decay_ls = jnp.exp(jnp.where(causal, seg, -jnp.inf))
    cb = jnp.einsum("bclgn,bcsgn->bcgls", cf, bf)
    y_diag = jnp.einsum("bcgrls,bcsgrp->bclgrp", cb[:, :, :, None] * decay_ls, xdt)
    decay_s = jnp.exp(a_cs[..., -1:] - a_cs)
    states = jnp.einsum("bcsgn,bcgrs,bcsgrp->bcgrpn", bf, decay_s, xdt)
    chunk_decay = jnp.exp(a_cs[..., -1])

    def step(h, inp):
        s_c, d_c = inp
        return h * d_c[..., None, None] + s_c, h

    h0 = jnp.zeros((bsz, SSD_GROUPS, r, hp, SSD_STATE), jnp.float32)
    _, prev = lax.scan(step, h0, (jnp.moveaxis(states, 1, 0), jnp.moveaxis(chunk_decay, 1, 0)))
    prev = jnp.moveaxis(prev, 0, 1)
    y_off = jnp.einsum("bclgn,bcgrpn,bcgrl->bclgrp", cf, prev, jnp.exp(a_cs))
    y = y_diag + y_off + xf * d_skip.astype(jnp.float32).reshape(SSD_GROUPS, r)[..., None]
    return y.reshape(bsz, seq, nh * hp).astype(x.dtype)


def sliding_window_gqa(q, k, v, sinks):
    bsz, seq, hq, hd = q.shape
    hkv = k.shape[2]
    r = hq // hkv
    blk = ATTN_BLOCK
    nb = seq // blk
    qb = q.astype(jnp.float32).reshape(bsz, nb, blk, hkv, r, hd)

    def with_prev(t):
        tb = t.reshape(bsz, nb, blk, hkv, hd)
        tp = jnp.pad(tb, ((0, 0), (1, 0), (0, 0), (0, 0), (0, 0)))[:, :-1]
        return jnp.concatenate([tp, tb], axis=2)

    kc = with_prev(k).astype(jnp.float32)
    vc = with_prev(v)
    s = jnp.einsum("bnqhrd,bnkhd->bnhrqk", qb, kc) * (1.0 / math.sqrt(hd))
    qi = jnp.arange(blk)[:, None]
    ki = jnp.arange(2 * blk)[None, :]
    diff = qi + blk - ki
    band = (diff >= 0) & (diff < WINDOW)
    kpos = jnp.arange(nb)[:, None, None] * blk + ki[None] - blk
    mask = band[None] & (kpos >= 0)
    s = jnp.where(mask[None, :, None, None], s, -jnp.inf)
    sink = jnp.broadcast_to(sinks.astype(jnp.float32).reshape(1, 1, hkv, r, 1, 1), s.shape[:-1] + (1,))
    p = jax.nn.softmax(jnp.concatenate([s, sink], axis=-1), axis=-1)[..., :-1]
    o = jnp.einsum("bnhrqk,bnkhd->bnqhrd", p.astype(v.dtype), vc)
    return o.reshape(bsz, seq, hq * hd)


def conformer_conv(u, dw_w, dw_b, ln_w, ln_b):
    a, g = jnp.split(u, 2, axis=-1)
    h = a * jax.nn.sigmoid(g)
    h = causal_depthwise_conv(h, dw_w, dw_b)
    return jax.nn.silu(layer_norm(h, ln_w, ln_b))


def hybrid_layer(x, cos, sin, norm_mix_w, w_in, ssd_conv_w, ssd_conv_b, ssd_dt_bias, ssd_a_log,
                 ssd_d, ssd_norm_w, q_norm_w, k_norm_w, attn_sinks, cm_dw_w, cm_dw_b,
                 cm_ln_w, cm_ln_b, w_out, norm_mlp_w, w_mlp_up, w_mlp_down):
    bsz, seq, _ = x.shape
    h = rms_norm(x, norm_mix_w)
    u = h @ w_in
    z, xbc, dt_raw, q, k, v, glu = _split(u, IN_SIZES)
    xbc = jax.nn.silu(causal_depthwise_conv(xbc, ssd_conv_w, ssd_conv_b))
    xs, bm, cm = _split(xbc, (SSD_WIDTH, SSD_GROUPS * SSD_STATE, SSD_GROUPS * SSD_STATE))
    dt = jax.nn.softplus(dt_raw.astype(jnp.float32) + ssd_dt_bias.astype(jnp.float32))
    a = -jnp.exp(ssd_a_log.astype(jnp.float32))
    y_ssd = ssd_chunked(xs.reshape(bsz, seq, SSD_HEADS, SSD_HEAD_DIM), dt, a,
                        bm.reshape(bsz, seq, SSD_GROUPS, SSD_STATE),
                        cm.reshape(bsz, seq, SSD_GROUPS, SSD_STATE), ssd_d)
    y_ssd = gated_rms_norm(y_ssd, z, ssd_norm_w)
    q = rms_norm(q.reshape(bsz, seq, ATTN_Q_HEADS, ATTN_HEAD_DIM), q_norm_w)
    k = rms_norm(k.reshape(bsz, seq, ATTN_KV_HEADS, ATTN_HEAD_DIM), k_norm_w)
    q = apply_rope(q, cos, sin)
    k = apply_rope(k, cos, sin)
    y_attn = sliding_window_gqa(q, k, v.reshape(bsz, seq, ATTN_KV_HEADS, ATTN_HEAD_DIM), attn_sinks)
    y_conv = conformer_conv(glu, cm_dw_w, cm_dw_b, cm_ln_w, cm_ln_b)
    x = x + jnp.concatenate([y_ssd, y_attn, y_conv], axis=-1) @ w_out
    hm = rms_norm(x, norm_mlp_w)
    x = x + jnp.square(jax.nn.relu(hm @ w_mlp_up)) @ w_mlp_down
    return x


def setup_inputs(seed: int = 0) -> dict:
    key = jax.random.key(seed)
    ks = jax.random.split(key, 24)
    f32 = jnp.float32
    nrm = lambda k, shp, scale: jax.random.normal(k, shp, f32) * scale
    dt_init = jnp.exp(jax.random.uniform(ks[5], (DEPTH, SSD_HEADS), f32)
                      * (math.log(0.1) - math.log(0.001)) + math.log(0.001))
    return {
        "x": nrm(ks[0], (BATCH, SEQ, D_MODEL), 1.0),
        "norm_mix_w": 1.0 + nrm(ks[1], (DEPTH, D_MODEL), 0.02),
        "w_in": nrm(ks[2], (DEPTH, D_MODEL, N_IN), D_MODEL ** -0.5),
        "ssd_conv_w": nrm(ks[3], (DEPTH, SSD_CONV, SSD_XBC), SSD_CONV ** -0.5),
        "ssd_conv_b": nrm(ks[4], (DEPTH, SSD_XBC), 0.02),
        "ssd_dt_bias": dt_init + jnp.log(-jnp.expm1(-dt_init)),
        "ssd_a_log": jnp.log(jax.random.uniform(ks[6], (DEPTH, SSD_HEADS), f32, 1.0, 16.0)),
        "ssd_d": 1.0 + nrm(ks[7], (DEPTH, SSD_HEADS), 0.02),
        "ssd_norm_w": 1.0 + nrm(ks[8], (DEPTH, SSD_WIDTH), 0.02),
        "q_norm_w": 1.0 + nrm(ks[9], (DEPTH, ATTN_HEAD_DIM), 0.02),
        "k_norm_w": 1.0 + nrm(ks[10], (DEPTH, ATTN_HEAD_DIM), 0.02),
        "attn_sinks": nrm(ks[11], (DEPTH, ATTN_Q_HEADS), 0.5),
        "cm_dw_w": nrm(ks[12], (DEPTH, CM_CONV_WIDTH, CM_CHANNELS), CM_CONV_WIDTH ** -0.5),
        "cm_dw_b": nrm(ks[13], (DEPTH, CM_CHANNELS), 0.02),
        "cm_ln_w": 1.0 + nrm(ks[14], (DEPTH, CM_CHANNELS), 0.02),
        "cm_ln_b": nrm(ks[15], (DEPTH, CM_CHANNELS), 0.02),
        "w_out": nrm(ks[16], (DEPTH, D_MIX, D_MODEL), D_MIX ** -0.5),
        "norm_mlp_w": 1.0 + nrm(ks[17], (DEPTH, D_MODEL), 0.02),
        "w_mlp_up": nrm(ks[18], (DEPTH, D_MODEL, D_FF), D_MODEL ** -0.5),
        "w_mlp_down": nrm(ks[19], (DEPTH, D_FF, D_MODEL), D_FF ** -0.5),
    }


def reference(x, norm_mix_w, w_in, ssd_conv_w, ssd_conv_b, ssd_dt_bias, ssd_a_log, ssd_d,
              ssd_norm_w, q_norm_w, k_norm_w, attn_sinks, cm_dw_w, cm_dw_b, cm_ln_w, cm_ln_b,
              w_out, norm_mlp_w, w_mlp_up, w_mlp_down):
    cos, sin = rope_tables(x.shape[1], ATTN_HEAD_DIM)
    for i in range(DEPTH):
        x = hybrid_layer(x, cos, sin, norm_mix_w[i], w_in[i], ssd_conv_w[i], ssd_conv_b[i],
                         ssd_dt_bias[i], ssd_a_log[i], ssd_d[i], ssd_norm_w[i], q_norm_w[i],
                         k_norm_w[i], attn_sinks[i], cm_dw_w[i], cm_dw_b[i], cm_ln_w[i],
                         cm_ln_b[i], w_out[i], norm_mlp_w[i], w_mlp_up[i], w_mlp_down[i])
    return x
```

```python
import functools
import math

import jax
import jax.numpy as jnp
from jax import lax
from jax.experimental import pallas as pl
from jax.experimental.pallas import tpu as pltpu

F32 = jnp.float32
BF16 = jnp.bfloat16

D_MODEL = 1024
D_MIX = 2 * D_MODEL
SSD_WIDTH = D_MIX // 2
ATTN_WIDTH = D_MIX // 4
CM_CHANNELS = D_MIX // 4
SSD_HEAD_DIM = 64
SSD_HEADS = SSD_WIDTH // SSD_HEAD_DIM
SSD_STATE = 128
SSD_GROUPS = 2
SSD_CONV = 4
SSD_CHUNK = 128
SSD_BC = SSD_GROUPS * SSD_STATE
SSD_XBC = SSD_WIDTH + 2 * SSD_BC
ATTN_HEAD_DIM = 64
ATTN_Q_HEADS = ATTN_WIDTH // ATTN_HEAD_DIM
ATTN_KV_HEADS = 2
ATTN_KV_WIDTH = ATTN_KV_HEADS * ATTN_HEAD_DIM
WINDOW = 128
ROPE_THETA = 10000.0
CM_CONV_WIDTH = 31
D_FF = 4 * D_MODEL
RMS_EPS = 1e-6
LN_EPS = 1e-5
IN_SIZES = (SSD_WIDTH, SSD_XBC, SSD_HEADS, ATTN_WIDTH, ATTN_KV_WIDTH, ATTN_KV_WIDTH, 2 * CM_CHANNELS)

LANES = 128
SUBLANES = 8
HEAD_PAD = LANES
VMEM_LIMIT = 56 * 1024 * 1024
MASK_VALUE = -1e30
SSD_TAIL = SUBLANES
CM_TAIL = 32


def _silu(x):
    return x * jax.nn.sigmoid(x)


def _softplus(x):
    return jnp.maximum(x, 0.0) + jnp.log1p(jnp.exp(-jnp.abs(x)))


def _split3(x):
    hi = x.astype(BF16)
    r1 = x - hi.astype(F32)
    mid = r1.astype(BF16)
    lo = (r1 - mid.astype(F32)).astype(BF16)
    return hi, mid, lo


def _dot(a, b):
    return jnp.dot(a, b, preferred_element_type=F32)


def _dot_nt(a, b):
    return lax.dot_general(a, b, (((1,), (1,)), ((), ())), preferred_element_type=F32)


def _dot_tn(a, b):
    return lax.dot_general(a, b, (((0,), (0,)), ((), ())), preferred_element_type=F32)


def _inproj_kernel(x_ref, nw_ref, wz_ref, wxbc_ref, wdt_ref, wq_ref, wk_ref, wv_ref, wglu_ref,
                   z_ref, xbc_ref, dt_ref, q_ref, k_ref, v_ref, glu_ref):
    x = x_ref[...]
    ms = jnp.mean(x * x, axis=-1, keepdims=True)
    h = (x * lax.rsqrt(ms + RMS_EPS) * nw_ref[...]).astype(BF16)
    z_ref[...] = _dot(h, wz_ref[...]).astype(z_ref.dtype)
    xbc_ref[...] = _dot(h, wxbc_ref[...]).astype(xbc_ref.dtype)
    dt_ref[...] = _dot(h, wdt_ref[...])
    q_ref[...] = _dot(h, wq_ref[...]).astype(q_ref.dtype)
    k_ref[...] = _dot(h, wk_ref[...]).astype(k_ref.dtype)
    v_ref[...] = _dot(h, wv_ref[...]).astype(v_ref.dtype)
    glu_ref[...] = _dot(h, wglu_ref[...]).astype(glu_ref.dtype)


def _const_spec(shape):
    return pl.BlockSpec(shape, lambda *_: (0,) * len(shape))


def _in_projection(x2d, norm_w, w_parts, tm):
    t = x2d.shape[0]
    widths = [w.shape[1] for w in w_parts]
    dtypes = [BF16, BF16, F32, BF16, BF16, BF16, BF16]
    row = lambda n: pl.BlockSpec((tm, n), lambda i: (i, 0))
    return pl.pallas_call(
        _inproj_kernel,
        grid=(t // tm,),
        in_specs=[row(D_MODEL), _const_spec((1, D_MODEL))] + [_const_spec(w.shape) for w in w_parts],
        out_specs=[row(n) for n in widths],
        out_shape=[jax.ShapeDtypeStruct((t, n), d) for n, d in zip(widths, dtypes)],
        compiler_params=pltpu.CompilerParams(
            dimension_semantics=("arbitrary",), vmem_limit_bytes=VMEM_LIMIT),
    )(x2d, norm_w, *w_parts)


def _ssd_kernel(xbc_ref, z_ref, dt_ref, cw_ref, cb_ref, dtb_ref, alog_ref, dexp_ref, nw_ref, e48_ref,
                y_ref, ext_ref, state_ref):
    q = SSD_CHUNK
    c = pl.program_id(1)

    @pl.when(c == 0)
    def _():
        ext_ref[0:SSD_TAIL, :] = jnp.zeros((SSD_TAIL, SSD_XBC), F32)
        state_ref[...] = jnp.zeros_like(state_ref)

    ext_ref[SSD_TAIL:SSD_TAIL + q, :] = xbc_ref[0].astype(F32)
    acc = jnp.broadcast_to(cb_ref[...], (q, SSD_XBC))
    for k in range(SSD_CONV):
        start = SSD_TAIL - (SSD_CONV - 1) + k
        acc = acc + cw_ref[k:k + 1, :] * ext_ref[start:start + q, :]
    ext_ref[0:SSD_TAIL, :] = ext_ref[q:q + SSD_TAIL, :]
    xbc = _silu(acc)
    xs = xbc[:, :SSD_WIDTH]
    bm = xbc[:, SSD_WIDTH:SSD_WIDTH + SSD_BC].astype(BF16)
    cm = xbc[:, SSD_WIDTH + SSD_BC:].astype(BF16)

    dt = _softplus(dt_ref[0] + dtb_ref[...])
    a_dt = dt * (-jnp.exp(alog_ref[...]))
    row_i = lax.broadcasted_iota(jnp.int32, (q, q), 0)
    col_i = lax.broadcasted_iota(jnp.int32, (q, q), 1)
    causal = row_i >= col_i
    tril = jnp.where(causal, 1.0, 0.0).astype(BF16)
    a_cs = _dot(jnp.concatenate([tril, tril, tril], axis=1),
                jnp.concatenate(_split3(a_dt), axis=0))
    a_cs_t = a_cs.T
    a_last = a_cs[q - 1:q, :]

    lane = lax.broadcasted_iota(jnp.int32, (3 * q, HEAD_PAD), 1)
    scal = jnp.concatenate([dt, jnp.exp(a_cs), jnp.exp(a_last - a_cs)], axis=0)
    scal = jnp.where(lane < SSD_HEADS, scal, 0.0)
    hi, mid, lo = _split3(scal)
    packed = (hi.astype(F32) + pltpu.roll(mid.astype(F32), SSD_HEADS, 1)
              + pltpu.roll(lo.astype(F32), 2 * SSD_HEADS, 1)).astype(BF16)
    expanded = _dot(packed, e48_ref[...])
    dt_x = expanded[0:q]
    decay_out = expanded[q:2 * q]
    decay_end = expanded[2 * q:3 * q]

    xdt = xs * dt_x
    xdt_b = xdt.astype(BF16)
    xdt_end = (xdt * decay_end).astype(BF16)

    half = SSD_WIDTH // SSD_GROUPS
    heads_per_group = SSD_HEADS // SSD_GROUPS
    pair_lane = lax.broadcasted_iota(jnp.int32, (q, LANES), 1)
    y_blocks = []
    new_state = []
    for g in range(SSD_GROUPS):
        b_g = bm[:, g * SSD_STATE:(g + 1) * SSD_STATE]
        c_g = cm[:, g * SSD_STATE:(g + 1) * SSD_STATE]
        cb = _dot_nt(c_g, b_g)
        st_g = state_ref[:, g * half:(g + 1) * half]
        y_off = _dot(c_g, st_g.astype(BF16)) * decay_out[:, g * half:(g + 1) * half]
        for pr in range(heads_per_group // 2):
            ms_pair = []
            for hh in range(2):
                h = g * heads_per_group + 2 * pr + hh
                seg = a_cs[:, h:h + 1] - a_cs_t[h:h + 1, :]
                ms_pair.append((cb * jnp.where(causal, jnp.exp(seg), 0.0)).astype(BF16))
            lo_l = g * half + pr * LANES
            xp = xdt_b[:, lo_l:lo_l + LANES]
            rhs = jnp.concatenate([jnp.where(pair_lane < SSD_HEAD_DIM, xp, jnp.zeros_like(xp)),
                                   jnp.where(pair_lane >= SSD_HEAD_DIM, xp, jnp.zeros_like(xp))], axis=0)
            y_diag = _dot(jnp.concatenate(ms_pair, axis=1), rhs)
            y_blocks.append(y_diag + y_off[:, pr * LANES:(pr + 1) * LANES])
        new_state.append(st_g * decay_out[q - 1:q, g * half:(g + 1) * half]
                         + _dot_tn(b_g, xdt_end[:, g * half:(g + 1) * half]))
    for g in range(SSD_GROUPS):
        state_ref[:, g * half:(g + 1) * half] = new_state[g]

    y = jnp.concatenate(y_blocks, axis=1) + xs * dexp_ref[...]
    gated = y * _silu(z_ref[0].astype(F32))
    outs = []
    for g in range(SSD_GROUPS):
        gg = gated[:, g * half:(g + 1) * half]
        outs.append(gg * lax.rsqrt(jnp.mean(gg * gg, axis=-1, keepdims=True) + RMS_EPS))
    y_ref[0] = (jnp.concatenate(outs, axis=1) * nw_ref[...]).astype(y_ref.dtype)


def _ssd_mixer(xbc, z, dt, conv_w, conv_b, dt_bias, a_log, d_exp, norm_w, e48):
    bsz, seq, _ = xbc.shape
    q = SSD_CHUNK
    blk = lambda n: pl.BlockSpec((1, q, n), lambda b, c: (b, c, 0))
    return pl.pallas_call(
        _ssd_kernel,
        grid=(bsz, seq // q),
        in_specs=[blk(SSD_XBC), blk(SSD_WIDTH), blk(HEAD_PAD),
                  _const_spec((SSD_CONV, SSD_XBC)), _const_spec((1, SSD_XBC)),
                  _const_spec((1, HEAD_PAD)), _const_spec((1, HEAD_PAD)),
                  _const_spec((1, SSD_WIDTH)), _const_spec((1, SSD_WIDTH)),
                  _const_spec((HEAD_PAD, SSD_WIDTH))],
        out_specs=blk(SSD_WIDTH),
        out_shape=jax.ShapeDtypeStruct((bsz, seq, SSD_WIDTH), BF16),
        scratch_shapes=[pltpu.VMEM((q + SSD_TAIL, SSD_XBC), F32),
                        pltpu.VMEM((SSD_STATE, SSD_WIDTH), F32)],
        compiler_params=pltpu.CompilerParams(
            dimension_semantics=("arbitrary", "arbitrary"), vmem_limit_bytes=VMEM_LIMIT),
    )(xbc, z, dt, conv_w, conv_b, dt_bias, a_log, d_exp, norm_w, e48)


def _head_rms_rope(x, nw, cos, sin, seg_ones):
    hi, mid, _ = _split3(x * x)
    ms = _dot(jnp.concatenate([hi, mid], axis=1), seg_ones) * (1.0 / ATTN_HEAD_DIM)
    xn = x * lax.rsqrt(ms + RMS_EPS) * nw
    width = x.shape[1]
    lane = lax.broadcasted_iota(jnp.int32, x.shape, 1)
    hd2 = ATTN_HEAD_DIM // 2
    swapped = jnp.where((lane & (ATTN_HEAD_DIM - 1)) < hd2,
                        pltpu.roll(xn, width - hd2, 1), pltpu.roll(xn, hd2, 1))
    return xn * cos + swapped * sin


def _attn_kernel(q_ref, kc_ref, kp_ref, vc_ref, vp_ref, cq_ref, sq_ref, cp_ref, sp_ref,
                 qnw_ref, knw_ref, sink_ref, segq_ref, segk_ref, o_ref):
    blk = WINDOW
    i = pl.program_id(1)
    hd = ATTN_HEAD_DIM
    cq = cq_ref[...]
    sq = sq_ref[...]
    qn = _head_rms_rope(q_ref[0].astype(F32), qnw_ref[...], cq, sq, segq_ref[...])
    qn = (qn * (1.0 / math.sqrt(hd))).astype(BF16)
    kc = _head_rms_rope(kc_ref[0].astype(F32), knw_ref[...], cq[:, :ATTN_KV_WIDTH], sq[:, :ATTN_KV_WIDTH],
                        segk_ref[...])
    kp = _head_rms_rope(kp_ref[0].astype(F32), knw_ref[...], cp_ref[...], sp_ref[...], segk_ref[...])
    kk = jnp.concatenate([kp, kc], axis=0)
    vv = jnp.concatenate([vp_ref[0], vc_ref[0]], axis=0).astype(F32)

    lane = lax.broadcasted_iota(jnp.int32, (2 * blk, ATTN_KV_WIDTH), 1)
    low = lane < hd

    def place(t, g, slot):
        src = t if g == slot else pltpu.roll(t, hd, 1)
        return jnp.where(low if slot == 0 else ~low, src, 0.0).astype(BF16)

    qi = lax.broadcasted_iota(jnp.int32, (blk, 2 * blk), 0)
    ki = lax.broadcasted_iota(jnp.int32, (blk, 2 * blk), 1)
    diff = qi + blk - ki
    first_key = jnp.where(i > 0, 0, blk)
    mask = (diff >= 0) & (diff < WINDOW) & (ki >= first_key)

    heads_per_kv = ATTN_Q_HEADS // ATTN_KV_HEADS
    for g in range(ATTN_KV_HEADS):
        k_slot = [place(kk, g, 0), place(kk, g, 1)]
        v_both = jnp.concatenate([place(vv, g, 0), place(vv, g, 1)], axis=0)
        for pr in range(heads_per_kv // 2):
            j = g * (heads_per_kv // 2) + pr
            q_pair = qn[:, j * LANES:(j + 1) * LANES]
            ps = []
            inv = []
            for hh in range(2):
                h = 2 * j + hh
                s = jnp.where(mask, _dot_nt(q_pair, k_slot[hh]), MASK_VALUE)
                sink = sink_ref[0:1, h:h + 1]
                m = jnp.maximum(jnp.max(s, axis=-1, keepdims=True), sink)
                p = jnp.exp(s - m)
                denom = jnp.sum(p, axis=-1, keepdims=True) + jnp.exp(sink - m)
                ps.append(p.astype(BF16))
                inv.append(1.0 / denom)
            o = _dot(jnp.concatenate(ps, axis=1), v_both)
            lane_o = lax.broadcasted_iota(jnp.int32, (blk, LANES), 1)
            o = o * jnp.where(lane_o < hd, inv[0], inv[1])
            o_ref[0, :, j * LANES:(j + 1) * LANES] = o.astype(o_ref.dtype)


def _attn_mixer(q, k, v, cos_q, sin_q, q_nw, k_nw, sinks, seg_q, seg_k):
    bsz, seq, _ = q.shape
    blk = WINDOW
    cur = lambda n: pl.BlockSpec((1, blk, n), lambda b, i: (b, i, 0))
    prev = lambda n: pl.BlockSpec((1, blk, n), lambda b, i: (b, jnp.maximum(i - 1, 0), 0))
    tab_cur = pl.BlockSpec((blk, ATTN_WIDTH), lambda b, i: (i, 0))
    tab_prev = pl.BlockSpec((blk, ATTN_KV_WIDTH), lambda b, i: (jnp.maximum(i - 1, 0), 0))
    return pl.pallas_call(
        _attn_kernel,
        grid=(bsz, seq // blk),
        in_specs=[cur(ATTN_WIDTH), cur(ATTN_KV_WIDTH), prev(ATTN_KV_WIDTH),
                  cur(ATTN_KV_WIDTH), prev(ATTN_KV_WIDTH),
                  tab_cur, tab_cur, tab_prev, tab_prev,
                  _const_spec((1, ATTN_WIDTH)), _const_spec((1, ATTN_KV_WIDTH)),
                  _const_spec((1, LANES)),
                  _const_spec((2 * ATTN_WIDTH, ATTN_WIDTH)), _const_spec((2 * ATTN_KV_WIDTH, ATTN_KV_WIDTH))],
        out_specs=cur(ATTN_WIDTH),
        out_shape=jax.ShapeDtypeStruct((bsz, seq, ATTN_WIDTH), BF16),
        compiler_params=pltpu.CompilerParams(
            dimension_semantics=("arbitrary", "arbitrary"), vmem_limit_bytes=VMEM_LIMIT),
    )(q, k, k, v, v, cos_q, sin_q, cos_q, sin_q, q_nw, k_nw, sinks, seg_q, seg_k)


def _cm_kernel(glu_ref, w_ref, b_ref, lnw_ref, lnb_ref, y_ref, ext_ref):
    q = SSD_CHUNK
    c = pl.program_id(1)

    @pl.when(c == 0)
    def _():
        ext_ref[0:CM_TAIL, :] = jnp.zeros((CM_TAIL, CM_CHANNELS), F32)

    u = glu_ref[0].astype(F32)
    ext_ref[CM_TAIL:CM_TAIL + q, :] = u[:, :CM_CHANNELS] * jax.nn.sigmoid(u[:, CM_CHANNELS:])
    acc = jnp.broadcast_to(b_ref[...], (q, CM_CHANNELS))
    for k in range(CM_CONV_WIDTH):
        start = CM_TAIL - (CM_CONV_WIDTH - 1) + k
        acc = acc + w_ref[k:k + 1, :] * ext_ref[start:start + q, :]
    ext_ref[0:CM_TAIL, :] = ext_ref[q:q + CM_TAIL, :]
    mu = jnp.mean(acc, axis=-1, keepdims=True)
    cen = acc - mu
    var = jnp.mean(cen * cen, axis=-1, keepdims=True)
    y = cen * lax.rsqrt(var + LN_EPS) * lnw_ref[...] + lnb_ref[...]
    y_ref[0] = _silu(y).astype(y_ref.dtype)


def _cm_mixer(glu, dw_w, dw_b, ln_w, ln_b):
    bsz, seq, _ = glu.shape
    q = SSD_CHUNK
    blk = lambda n: pl.BlockSpec((1, q, n), lambda b, c: (b, c, 0))
    return pl.pallas_call(
        _cm_kernel,
        grid=(bsz, seq // q),
        in_specs=[blk(2 * CM_CHANNELS), _const_spec((CM_CONV_WIDTH, CM_CHANNELS)),
                  _const_spec((1, CM_CHANNELS)), _const_spec((1, CM_CHANNELS)), _const_spec((1, CM_CHANNELS))],
        out_specs=blk(CM_CHANNELS),
        out_shape=jax.ShapeDtypeStruct((bsz, seq, CM_CHANNELS), BF16),
        scratch_shapes=[pltpu.VMEM((q + CM_TAIL, CM_CHANNELS), F32)],
        compiler_params=pltpu.CompilerParams(
            dimension_semantics=("arbitrary", "arbitrary"), vmem_limit_bytes=VMEM_LIMIT),
    )(glu, dw_w, dw_b, ln_w, ln_b)


def _out_mlp_kernel(x_ref, ys_ref, ya_ref, yc_ref, wos_ref, woa_ref, woc_ref, nw_ref, wup_ref, wdn_ref,
                    o_ref, x1_ref, hm_ref):
    f = pl.program_id(1)

    @pl.when(f == 0)
    def _():
        x1 = (x_ref[...] + _dot(ys_ref[...], wos_ref[...]) + _dot(ya_ref[...], woa_ref[...])
              + _dot(yc_ref[...], woc_ref[...]))
        x1_ref[...] = x1
        ms = jnp.mean(x1 * x1, axis=-1, keepdims=True)
        hm_ref[...] = (x1 * lax.rsqrt(ms + RMS_EPS) * nw_ref[...]).astype(BF16)

    up = _dot(hm_ref[...], wup_ref[...])
    act = jnp.square(jnp.maximum(up, 0.0)).astype(BF16)
    x1_ref[...] += _dot(act, wdn_ref[...])

    @pl.when(f == pl.num_programs(1) - 1)
    def _():
        o_ref[...] = x1_ref[...]


def _out_mlp(x2d, y_ssd, y_attn, y_cm, wo_s, wo_a, wo_c, norm_w, w_up, w_down, tm, tf):
    t = x2d.shape[0]
    row = lambda n: pl.BlockSpec((tm, n), lambda i, f: (i, 0))
    return pl.pallas_call(
        _out_mlp_kernel,
        grid=(t // tm, D_FF // tf),
        in_specs=[row(D_MODEL), row(SSD_WIDTH), row(ATTN_WIDTH), row(CM_CHANNELS),
                  _const_spec(wo_s.shape), _const_spec(wo_a.shape), _const_spec(wo_c.shape),
                  _const_spec((1, D_MODEL)),
                  pl.BlockSpec((D_MODEL, tf), lambda i, f: (0, f)),
                  pl.BlockSpec((tf, D_MODEL), lambda i, f: (f, 0))],
        out_specs=row(D_MODEL),
        out_shape=jax.ShapeDtypeStruct((t, D_MODEL), F32),
        scratch_shapes=[pltpu.VMEM((tm, D_MODEL), F32), pltpu.VMEM((tm, D_MODEL), BF16)],
        compiler_params=pltpu.CompilerParams(
            dimension_semantics=("arbitrary", "arbitrary"), vmem_limit_bytes=VMEM_LIMIT),
    )(x2d, y_ssd, y_attn, y_cm, wo_s, wo_a, wo_c, norm_w, w_up, w_down)


def _row_tile(t, want):
    tm = min(t, want)
    assert t % tm == 0
    return tm


def _rope_tables(seq):
    inv_freq = ROPE_THETA ** (-jnp.arange(0, ATTN_HEAD_DIM, 2, dtype=F32) / ATTN_HEAD_DIM)
    ang = jnp.arange(seq, dtype=F32)[:, None] * inv_freq[None, :]
    cos, sin = jnp.cos(ang), jnp.sin(ang)
    cos_h = jnp.concatenate([cos, cos], axis=-1)
    sin_h = jnp.concatenate([-sin, sin], axis=-1)
    return jnp.tile(cos_h, (1, ATTN_Q_HEADS)), jnp.tile(sin_h, (1, ATTN_Q_HEADS))


def _pad_lanes(v, width):
    return jnp.pad(v, ((0, 0), (0, width - v.shape[-1])))


def _segment_ones(width):
    r = jnp.arange(2 * width) % width
    c = jnp.arange(width)
    return ((r[:, None] // ATTN_HEAD_DIM) == (c[None, :] // ATTN_HEAD_DIM)).astype(BF16)


def _expand48():
    r = jnp.arange(HEAD_PAD)
    c = jnp.arange(SSD_WIDTH)
    m = (r[:, None] < 3 * SSD_HEADS) & ((r[:, None] % SSD_HEADS) == (c[None, :] // SSD_HEAD_DIM))
    return m.astype(BF16)


def kernel(x, norm_mix_w, w_in, ssd_conv_w, ssd_conv_b, ssd_dt_bias, ssd_a_log, ssd_d, ssd_norm_w,
           q_norm_w, k_norm_w, attn_sinks, cm_dw_w, cm_dw_b, cm_ln_w, cm_ln_b, w_out, norm_mlp_w,
           w_mlp_up, w_mlp_down):
    bsz, seq, _ = x.shape
    depth = w_in.shape[0]
    t = bsz * seq
    tm_in = _row_tile(t, 512)
    tm_mlp = _row_tile(t, 1024)
    tf = 1024

    cos_q, sin_q = _rope_tables(seq)
    seg_q = _segment_ones(ATTN_WIDTH)
    seg_k = _segment_ones(ATTN_KV_WIDTH)
    e48 = _expand48()

    offs = [0]
    for s in IN_SIZES:
        offs.append(offs[-1] + s)

    x2d = x.reshape(t, D_MODEL)
    for i in range(depth):
        w_i = w_in[i].astype(BF16)
        w_parts = [w_i[:, offs[j]:offs[j + 1]] for j in range(len(IN_SIZES))]
        w_parts[2] = _pad_lanes(w_parts[2], HEAD_PAD)
        z, xbc, dt, q, k, v, glu = _in_projection(x2d, norm_mix_w[i][None, :], w_parts, tm_in)

        shp = lambda a: a.reshape(bsz, seq, a.shape[-1])
        y_ssd = _ssd_mixer(
            shp(xbc), shp(z), shp(dt), ssd_conv_w[i], ssd_conv_b[i][None, :],
            _pad_lanes(ssd_dt_bias[i][None, :], HEAD_PAD), _pad_lanes(ssd_a_log[i][None, :], HEAD_PAD),
            jnp.repeat(ssd_d[i], SSD_HEAD_DIM)[None, :], ssd_norm_w[i][None, :], e48)
        y_attn = _attn_mixer(
            shp(q), shp(k), shp(v), cos_q, sin_q,
            jnp.tile(q_norm_w[i], ATTN_Q_HEADS)[None, :], jnp.tile(k_norm_w[i], ATTN_KV_HEADS)[None, :],
            _pad_lanes(attn_sinks[i][None, :], LANES), seg_q, seg_k)
        y_cm = _cm_mixer(shp(glu), cm_dw_w[i], cm_dw_b[i][None, :], cm_ln_w[i][None, :], cm_ln_b[i][None, :])

        w_o = w_out[i].astype(BF16)
        x2d = _out_mlp(
            x2d, y_ssd.reshape(t, SSD_WIDTH), y_attn.reshape(t, ATTN_WIDTH), y_cm.reshape(t, CM_CHANNELS),
            w_o[:SSD_WIDTH], w_o[SSD_WIDTH:SSD_WIDTH + ATTN_WIDTH], w_o[SSD_WIDTH + ATTN_WIDTH:],
            norm_mlp_w[i][None, :], w_mlp_up[i].astype(BF16), w_mlp_down[i].astype(BF16), tm_mlp, tf)
    return x2d.reshape(bsz, seq, D_MODEL)
```

```python
import functools
import math

import jax
import jax.numpy as jnp
from jax import lax
from jax.experimental import pallas as pl
from jax.experimental.pallas import tpu as pltpu

F32 = jnp.float32
BF16 = jnp.bfloat16

D_MODEL = 1024
D_MIX = 2 * D_MODEL
SSD_WIDTH = D_MIX // 2
ATTN_WIDTH = D_MIX // 4
CM_CHANNELS = D_MIX // 4
SSD_HEAD_DIM = 64
SSD_HEADS = SSD_WIDTH // SSD_HEAD_DIM
SSD_STATE = 128
SSD_GROUPS = 2
SSD_CONV = 4
SSD_CHUNK = 128
SSD_BC = SSD_GROUPS * SSD_STATE
SSD_XBC = SSD_WIDTH + 2 * SSD_BC
ATTN_HEAD_DIM = 64
ATTN_Q_HEADS = ATTN_WIDTH // ATTN_HEAD_DIM
ATTN_KV_HEADS = 2
ATTN_KV_WIDTH = ATTN_KV_HEADS * ATTN_HEAD_DIM
WINDOW = 128
ROPE_THETA = 10000.0
CM_CONV_WIDTH = 31
D_FF = 4 * D_MODEL
RMS_EPS = 1e-6
LN_EPS = 1e-5
IN_SIZES = (SSD_WIDTH, SSD_XBC, SSD_HEADS, ATTN_WIDTH, ATTN_KV_WIDTH, ATTN_KV_WIDTH, 2 * CM_CHANNELS)

LANES = 128
SUBLANES = 8
HEAD_PAD = LANES
VMEM_LIMIT = 56 * 1024 * 1024
MASK_VALUE = -1e30
SSD_TAIL = SUBLANES
CM_TAIL = 32


def _silu(x):
    return x * jax.nn.sigmoid(x)


def _softplus(x):
    return jnp.maximum(x, 0.0) + jnp.log1p(jnp.exp(-jnp.abs(x)))


def _split3(x):
    hi = x.astype(BF16)
    r1 = x - hi.astype(F32)
    mid = r1.astype(BF16)
    lo = (r1 - mid.astype(F32)).astype(BF16)
    return hi, mid, lo


def _dot(a, b):
    return jnp.dot(a, b, preferred_element_type=F32)


def _dot_nt(a, b):
    return lax.dot_general(a, b, (((1,), (1,)), ((), ())), preferred_element_type=F32)


def _dot_tn(a, b):
    return lax.dot_general(a, b, (((0,), (0,)), ((), ())), preferred_element_type=F32)


def _inproj_kernel(x_ref, nw_ref, wz_ref, wxbc_ref, wdt_ref, wq_ref, wk_ref, wv_ref, wglu_ref,
                   z_ref, xbc_ref, dt_ref, q_ref, k_ref, v_ref, glu_ref):
    x = x_ref[...]
    ms = jnp.mean(x * x, axis=-1, keepdims=True)
    h = (x * lax.rsqrt(ms + RMS_EPS) * nw_ref[...]).astype(BF16)
    z_ref[...] = _dot(h, wz_ref[...]).astype(z_ref.dtype)
    xbc_ref[...] = _dot(h, wxbc_ref[...]).astype(xbc_ref.dtype)
    dt_ref[...] = _dot(h, wdt_ref[...])
    q_ref[...] = _dot(h, wq_ref[...]).astype(q_ref.dtype)
    k_ref[...] = _dot(h, wk_ref[...]).astype(k_ref.dtype)
    v_ref[...] = _dot(h, wv_ref[...]).astype(v_ref.dtype)
    glu_ref[...] = _dot(h, wglu_ref[...]).astype(glu_ref.dtype)


def _const_spec(shape):
    return pl.BlockSpec(shape, lambda *_: (0,) * len(shape))


def _in_projection(x2d, norm_w, w_parts, tm):
    t = x2d.shape[0]
    widths = [w.shape[1] for w in w_parts]
    dtypes = [BF16, BF16, F32, BF16, BF16, BF16, BF16]
    row = lambda n: pl.BlockSpec((tm, n), lambda i: (i, 0))
    return pl.pallas_call(
        _inproj_kernel,
        grid=(t // tm,),
        in_specs=[row(D_MODEL), _const_spec((1, D_MODEL))] + [_const_spec(w.shape) for w in w_parts],
        out_specs=[row(n) for n in widths],
        out_shape=[jax.ShapeDtypeStruct((t, n), d) for n, d in zip(widths, dtypes)],
        compiler_params=pltpu.CompilerParams(
            dimension_semantics=("arbitrary",), vmem_limit_bytes=VMEM_LIMIT),
    )(x2d, norm_w, *w_parts)


def _ssd_kernel(xbc_ref, z_ref, dt_ref, cw_ref, cb_ref, dtb_ref, alog_ref, dexp_ref, nw_ref, e48_ref,
                y_ref, ext_ref, state_ref):
    q = SSD_CHUNK
    c = pl.program_id(1)

    @pl.when(c == 0)
    def _():
        ext_ref[0:SSD_TAIL, :] = jnp.zeros((SSD_TAIL, SSD_XBC), F32)
        state_ref[...] = jnp.zeros_like(state_ref)

    ext_ref[SSD_TAIL:SSD_TAIL + q, :] = xbc_ref[0].astype(F32)
    acc = jnp.broadcast_to(cb_ref[...], (q, SSD_XBC))
    for k in range(SSD_CONV):
        start = SSD_TAIL - (SSD_CONV - 1) + k
        acc = acc + cw_ref[k:k + 1, :] * ext_ref[start:start + q, :]
    ext_ref[0:SSD_TAIL, :] = ext_ref[q:q + SSD_TAIL, :]
    xbc = _silu(acc)
    xs = xbc[:, :SSD_WIDTH]
    bm = xbc[:, SSD_WIDTH:SSD_WIDTH + SSD_BC].astype(BF16)
    cm = xbc[:, SSD_WIDTH + SSD_BC:].astype(BF16)

    dt = _softplus(dt_ref[0] + dtb_ref[...])
    a_dt = dt * (-jnp.exp(alog_ref[...]))
    row_i = lax.broadcasted_iota(jnp.int32, (q, q), 0)
    col_i = lax.broadcasted_iota(jnp.int32, (q, q), 1)
    causal = row_i >= col_i
    tril = jnp.where(causal, 1.0, 0.0).astype(BF16)
    a_cs = _dot(jnp.concatenate([tril, tril, tril], axis=1),
                jnp.concatenate(_split3(a_dt), axis=0))
    a_cs_t = a_cs.T
    a_last = a_cs[q - 1:q, :]

    lane = lax.broadcasted_iota(jnp.int32, (3 * q, HEAD_PAD), 1)
    scal = jnp.concatenate([dt, jnp.exp(a_cs), jnp.exp(a_last - a_cs)], axis=0)
    scal = jnp.where(lane < SSD_HEADS, scal, 0.0)
    hi, mid, lo = _split3(scal)
    packed = (hi.astype(F32) + pltpu.roll(mid.astype(F32), SSD_HEADS, 1)
              + pltpu.roll(lo.astype(F32), 2 * SSD_HEADS, 1)).astype(BF16)
    expanded = _dot(packed, e48_ref[...])
    dt_x = expanded[0:q]
    decay_out = expanded[q:2 * q]
    decay_end = expanded[2 * q:3 * q]

    xdt = xs * dt_x
    xdt_b = xdt.astype(BF16)
    xdt_end = (xdt * decay_end).astype(BF16)

    half = SSD_WIDTH // SSD_GROUPS
    heads_per_group = SSD_HEADS // SSD_GROUPS
    pair_lane = lax.broadcasted_iota(jnp.int32, (q, LANES), 1)
    y_blocks = []
    new_state = []
    for g in range(SSD_GROUPS):
        b_g = bm[:, g * SSD_STATE:(g + 1) * SSD_STATE]
        c_g = cm[:, g * SSD_STATE:(g + 1) * SSD_STATE]
        cb = _dot_nt(c_g, b_g)
        st_g = state_ref[:, g * half:(g + 1) * half]
        y_off = _dot(c_g, st_g.astype(BF16)) * decay_out[:, g * half:(g + 1) * half]
        for pr in range(heads_per_group // 2):
            ms_pair = []
            for hh in range(2):
                h = g * heads_per_group + 2 * pr + hh
                seg = a_cs[:, h:h + 1] - a_cs_t[h:h + 1, :]
                ms_pair.append((cb * jnp.where(causal, jnp.exp(seg), 0.0)).astype(BF16))
            lo_l = g * half + pr * LANES
            xp = xdt_b[:, lo_l:lo_l + LANES]
            rhs = jnp.concatenate([jnp.where(pair_lane < SSD_HEAD_DIM, xp, jnp.zeros_like(xp)),
                                   jnp.where(pair_lane >= SSD_HEAD_DIM, xp, jnp.zeros_like(xp))], axis=0)
            y_diag = _dot(jnp.concatenate(ms_pair, axis=1), rhs)
            y_blocks.append(y_diag + y_off[:, pr * LANES:(pr + 1) * LANES])
        new_state.append(st_g * decay_out[q - 1:q, g * half:(g + 1) * half]
                         + _dot_tn(b_g, xdt_end[:, g * half:(g + 1) * half]))
    for g in range(SSD_GROUPS):
        state_ref[:, g * half:(g + 1) * half] = new_state[g]

    y = jnp.concatenate(y_blocks, axis=1) + xs * dexp_ref[...]
    gated = y * _silu(z_ref[0].astype(F32))
    outs = []
    for g in range(SSD_GROUPS):
        gg = gated[:, g * half:(g + 1) * half]
        outs.append(gg * lax.rsqrt(jnp.mean(gg * gg, axis=-1, keepdims=True) + RMS_EPS))
    y_ref[0] = (jnp.concatenate(outs, axis=1) * nw_ref[...]).astype(y_ref.dtype)


def _ssd_mixer(xbc, z, dt, conv_w, conv_b, dt_bias, a_log, d_exp, norm_w, e48):
    bsz, seq, _ = xbc.shape
    q = SSD_CHUNK
    blk = lambda n: pl.BlockSpec((1, q, n), lambda b, c: (b, c, 0))
    return pl.pallas_call(
        _ssd_kernel,
        grid=(bsz, seq // q),
        in_specs=[blk(SSD_XBC), blk(SSD_WIDTH), blk(HEAD_PAD),
                  _const_spec((SSD_CONV, SSD_XBC)), _const_spec((1, SSD_XBC)),
                  _const_spec((1, HEAD_PAD)), _const_spec((1, HEAD_PAD)),
                  _const_spec((1, SSD_WIDTH)), _const_spec((1, SSD_WIDTH)),
                  _const_spec((HEAD_PAD, SSD_WIDTH))],
        out_specs=blk(SSD_WIDTH),
        out_shape=jax.ShapeDtypeStruct((bsz, seq, SSD_WIDTH), BF16),
        scratch_shapes=[pltpu.VMEM((q + SSD_TAIL, SSD_XBC), F32),
                        pltpu.VMEM((SSD_STATE, SSD_WIDTH), F32)],
        compiler_params=pltpu.CompilerParams(
            dimension_semantics=("arbitrary", "arbitrary"), vmem_limit_bytes=VMEM_LIMIT),
    )(xbc, z, dt, conv_w, conv_b, dt_bias, a_log, d_exp, norm_w, e48)


def _head_rms_rope(x, nw, cos, sin, seg_ones):
    hi, mid, _ = _split3(x * x)
    ms = _dot(jnp.concatenate([hi, mid], axis=1), seg_ones) * (1.0 / ATTN_HEAD_DIM)
    xn = x * lax.rsqrt(ms + RMS_EPS) * nw
    width = x.shape[1]
    lane = lax.broadcasted_iota(jnp.int32, x.shape, 1)
    hd2 = ATTN_HEAD_DIM // 2
    swapped = jnp.where((lane & (ATTN_HEAD_DIM - 1)) < hd2,
                        pltpu.roll(xn, width - hd2, 1), pltpu.roll(xn, hd2, 1))
    return xn * cos + swapped * sin


ATTN_PREP_ROWS = 256
ATTN_SLOTS = ATTN_KV_HEADS * 2


def _attn_kernel(sink_ref, q_ref, k_ref, v_ref, cos_ref, sin_ref, qnw_ref, knw_ref, segq_ref, segk_ref,
                 bias_ref, o_ref, qn_ref, kp_ref, vp_ref):
    blk = WINDOW
    hd = ATTN_HEAD_DIM
    seq = q_ref.shape[1]
    rt = ATTN_PREP_ROWS

    lane_kv = lax.broadcasted_iota(jnp.int32, (rt, ATTN_KV_WIDTH), 1)
    low = lane_kv < hd

    def place(t, g, slot):
        src = t if g == slot else pltpu.roll(t, hd, 1)
        return jnp.where(low if slot == 0 else ~low, src, 0.0).astype(BF16)

    for n in range(ATTN_SLOTS):
        kp_ref[n, 0:blk, :] = jnp.zeros((blk, ATTN_KV_WIDTH), BF16)
        vp_ref[n, 0:blk, :] = jnp.zeros((blk, 2 * ATTN_KV_WIDTH), BF16)

    def prep(t, carry):
        r0 = pl.multiple_of(t * rt, rt)
        rows = pl.ds(r0, rt)
        dst = pl.ds(r0 + blk, rt)
        cos = cos_ref[rows, :]
        sin = sin_ref[rows, :]
        qn_ref[rows, :] = _head_rms_rope(q_ref[0, rows, :].astype(F32), qnw_ref[...], cos, sin,
                                         segq_ref[...]).astype(BF16)
        kn = _head_rms_rope(k_ref[0, rows, :].astype(F32), knw_ref[...], cos[:, :ATTN_KV_WIDTH],
                            sin[:, :ATTN_KV_WIDTH], segk_ref[...])
        vv = v_ref[0, rows, :].astype(F32)
        for g in range(ATTN_KV_HEADS):
            for slot in range(2):
                kp_ref[2 * g + slot, dst, :] = place(kn, g, slot)
                ones = jnp.where(low if slot == 0 else ~low, 1.0, 0.0).astype(BF16)
                vp_ref[2 * g + slot, dst, :] = jnp.concatenate([place(vv, g, slot), ones], axis=1)
        return carry

    lax.fori_loop(0, seq // rt, prep, 0)

    lane_o = lax.broadcasted_iota(jnp.int32, (blk, LANES), 1)
    heads_per_kv = ATTN_Q_HEADS // ATTN_KV_HEADS

    def block(i, carry):
        r0 = pl.multiple_of(i * blk, blk)
        qrows = pl.ds(r0, blk)
        win = pl.ds(r0, 2 * blk)
        bias = bias_ref[jnp.minimum(i, 1)]
        for g in range(ATTN_KV_HEADS):
            v_both = jnp.concatenate([vp_ref[2 * g, win, :], vp_ref[2 * g + 1, win, :]], axis=0)
            for pr in range(heads_per_kv // 2):
                j = g * (heads_per_kv // 2) + pr
                q_pair = qn_ref[qrows, j * LANES:(j + 1) * LANES]
                ps = []
                sink_terms = []
                for hh in range(2):
                    sink = sink_ref[2 * j + hh]
                    s = _dot_nt(q_pair, kp_ref[2 * g + hh, win, :]) + bias
                    m = jnp.maximum(jnp.max(s, axis=-1, keepdims=True), sink)
                    ps.append(jnp.exp(s - m).astype(BF16))
                    sink_terms.append(jnp.exp(sink - m))
                pv = _dot(jnp.concatenate(ps, axis=1), v_both)
                denom = pv[:, LANES:] + jnp.where(lane_o < hd, sink_terms[0], sink_terms[1])
                o_ref[0, qrows, j * LANES:(j + 1) * LANES] = (pv[:, :LANES] / denom).astype(o_ref.dtype)
        return carry

    lax.fori_loop(0, seq // blk, block, 0, unroll=2)


def _attn_mixer(q, k, v, cos_q, sin_q, q_nw, k_nw, sinks, seg_q, seg_k, bias):
    bsz, seq, _ = q.shape
    blk = WINDOW
    full = lambda n: pl.BlockSpec((1, seq, n), lambda b: (b, 0, 0))
    return pl.pallas_call(
        _attn_kernel,
        grid=(bsz,),
        in_specs=[pl.BlockSpec(memory_space=pltpu.SMEM),
                  full(ATTN_WIDTH), full(ATTN_KV_WIDTH), full(ATTN_KV_WIDTH),
                  _const_spec((seq, ATTN_WIDTH)), _const_spec((seq, ATTN_WIDTH)),
                  _const_spec((1, ATTN_WIDTH)), _const_spec((1, ATTN_KV_WIDTH)),
                  _const_spec((2 * ATTN_WIDTH, ATTN_WIDTH)), _const_spec((2 * ATTN_KV_WIDTH, ATTN_KV_WIDTH)),
                  _const_spec((2, blk, 2 * blk))],
        out_specs=full(ATTN_WIDTH),
        out_shape=jax.ShapeDtypeStruct((bsz, seq, ATTN_WIDTH), BF16),
        scratch_shapes=[pltpu.VMEM((seq, ATTN_WIDTH), BF16),
                        pltpu.VMEM((ATTN_SLOTS, seq + blk, ATTN_KV_WIDTH), BF16),
                        pltpu.VMEM((ATTN_SLOTS, seq + blk, 2 * ATTN_KV_WIDTH), BF16)],
        compiler_params=pltpu.CompilerParams(
            dimension_semantics=("arbitrary",), vmem_limit_bytes=VMEM_LIMIT),
    )(sinks, q, k, v, cos_q, sin_q, q_nw, k_nw, seg_q, seg_k, bias)


def _cm_kernel(glu_ref, w_ref, b_ref, lnw_ref, lnb_ref, y_ref, ext_ref):
    q = SSD_CHUNK
    c = pl.program_id(1)

    @pl.when(c == 0)
    def _():
        ext_ref[0:CM_TAIL, :] = jnp.zeros((CM_TAIL, CM_CHANNELS), F32)

    u = glu_ref[0].astype(F32)
    ext_ref[CM_TAIL:CM_TAIL + q, :] = u[:, :CM_CHANNELS] * jax.nn.sigmoid(u[:, CM_CHANNELS:])
    off = CM_TAIL - (CM_CONV_WIDTH - 1)
    acc = jnp.broadcast_to(b_ref[...], (q, CM_CHANNELS))
    for r in range(SUBLANES):
        rows = q + SUBLANES if r else q
        part = None
        for m in range((off + CM_CONV_WIDTH) // SUBLANES + 1):
            k = SUBLANES * m + r - off
            if 0 <= k < CM_CONV_WIDTH:
                term = w_ref[k:k + 1, :] * ext_ref[SUBLANES * m:SUBLANES * m + rows, :]
                part = term if part is None else part + term
        if r:
            part = pltpu.roll(part, rows - r, 0)
        acc = acc + part[0:q]
    ext_ref[0:CM_TAIL, :] = ext_ref[q:q + CM_TAIL, :]
    mu = jnp.mean(acc, axis=-1, keepdims=True)
    cen = acc - mu
    var = jnp.mean(cen * cen, axis=-1, keepdims=True)
    y = cen * lax.rsqrt(var + LN_EPS) * lnw_ref[...] + lnb_ref[...]
    y_ref[0] = _silu(y).astype(y_ref.dtype)


def _cm_mixer(glu, dw_w, dw_b, ln_w, ln_b):
    bsz, seq, _ = glu.shape
    q = SSD_CHUNK
    blk = lambda n: pl.BlockSpec((1, q, n), lambda b, c: (b, c, 0))
    return pl.pallas_call(
        _cm_kernel,
        grid=(bsz, seq // q),
        in_specs=[blk(2 * CM_CHANNELS), _const_spec((CM_CONV_WIDTH, CM_CHANNELS)),
                  _const_spec((1, CM_CHANNELS)), _const_spec((1, CM_CHANNELS)), _const_spec((1, CM_CHANNELS))],
        out_specs=blk(CM_CHANNELS),
        out_shape=jax.ShapeDtypeStruct((bsz, seq, CM_CHANNELS), BF16),
        scratch_shapes=[pltpu.VMEM((q + CM_TAIL, CM_CHANNELS), F32)],
        compiler_params=pltpu.CompilerParams(
            dimension_semantics=("arbitrary", "arbitrary"), vmem_limit_bytes=VMEM_LIMIT),
    )(glu, dw_w, dw_b, ln_w, ln_b)


def _out_mlp_kernel(x_ref, ys_ref, ya_ref, yc_ref, wos_ref, woa_ref, woc_ref, nw_ref, wup_ref, wdn_ref,
                    o_ref, x1_ref, hm_ref):
    f = pl.program_id(1)

    @pl.when(f == 0)
    def _():
        x1 = (x_ref[...] + _dot(ys_ref[...], wos_ref[...]) + _dot(ya_ref[...], woa_ref[...])
              + _dot(yc_ref[...], woc_ref[...]))
        x1_ref[...] = x1
        ms = jnp.mean(x1 * x1, axis=-1, keepdims=True)
        hm_ref[...] = (x1 * lax.rsqrt(ms + RMS_EPS) * nw_ref[...]).astype(BF16)

    up = _dot(hm_ref[...], wup_ref[...])
    act = jnp.square(jnp.maximum(up, 0.0)).astype(BF16)
    x1_ref[...] += _dot(act, wdn_ref[...])

    @pl.when(f == pl.num_programs(1) - 1)
    def _():
        o_ref[...] = x1_ref[...]


def _out_mlp(x2d, y_ssd, y_attn, y_cm, wo_s, wo_a, wo_c, norm_w, w_up, w_down, tm, tf):
    t = x2d.shape[0]
    row = lambda n: pl.BlockSpec((tm, n), lambda i, f: (i, 0))
    return pl.pallas_call(
        _out_mlp_kernel,
        grid=(t // tm, D_FF // tf),
        in_specs=[row(D_MODEL), row(SSD_WIDTH), row(ATTN_WIDTH), row(CM_CHANNELS),
                  _const_spec(wo_s.shape), _const_spec(wo_a.shape), _const_spec(wo_c.shape),
                  _const_spec((1, D_MODEL)),
                  pl.BlockSpec((D_MODEL, tf), lambda i, f: (0, f)),
                  pl.BlockSpec((tf, D_MODEL), lambda i, f: (f, 0))],
        out_specs=row(D_MODEL),
        out_shape=jax.ShapeDtypeStruct((t, D_MODEL), F32),
        scratch_shapes=[pltpu.VMEM((tm, D_MODEL), F32), pltpu.VMEM((tm, D_MODEL), BF16)],
        compiler_params=pltpu.CompilerParams(
            dimension_semantics=("arbitrary", "arbitrary"), vmem_limit_bytes=VMEM_LIMIT),
    )(x2d, y_ssd, y_attn, y_cm, wo_s, wo_a, wo_c, norm_w, w_up, w_down)


def _row_tile(t, want):
    tm = min(t, want)
    assert t % tm == 0
    return tm


def _rope_tables(seq):
    inv_freq = ROPE_THETA ** (-jnp.arange(0, ATTN_HEAD_DIM, 2, dtype=F32) / ATTN_HEAD_DIM)
    ang = jnp.arange(seq, dtype=F32)[:, None] * inv_freq[None, :]
    cos, sin = jnp.cos(ang), jnp.sin(ang)
    cos_h = jnp.concatenate([cos, cos], axis=-1)
    sin_h = jnp.concatenate([-sin, sin], axis=-1)
    return jnp.tile(cos_h, (1, ATTN_Q_HEADS)), jnp.tile(sin_h, (1, ATTN_Q_HEADS))


def _pad_lanes(v, width):
    return jnp.pad(v, ((0, 0), (0, width - v.shape[-1])))


def _segment_ones(width):
    r = jnp.arange(2 * width) % width
    c = jnp.arange(width)
    return ((r[:, None] // ATTN_HEAD_DIM) == (c[None, :] // ATTN_HEAD_DIM)).astype(BF16)


def _band_bias():
    qi = jnp.arange(WINDOW)[:, None]
    ki = jnp.arange(2 * WINDOW)[None, :]
    diff = qi + WINDOW - ki
    band = (diff >= 0) & (diff < WINDOW)
    first = band & (ki >= WINDOW)
    return jnp.where(jnp.stack([first, band]), 0.0, MASK_VALUE).astype(F32)


def _expand48():
    r = jnp.arange(HEAD_PAD)
    c = jnp.arange(SSD_WIDTH)
    m = (r[:, None] < 3 * SSD_HEADS) & ((r[:, None] % SSD_HEADS) == (c[None, :] // SSD_HEAD_DIM))
    return m.astype(BF16)


def kernel(x, norm_mix_w, w_in, ssd_conv_w, ssd_conv_b, ssd_dt_bias, ssd_a_log, ssd_d, ssd_norm_w,
           q_norm_w, k_norm_w, attn_sinks, cm_dw_w, cm_dw_b, cm_ln_w, cm_ln_b, w_out, norm_mlp_w,
           w_mlp_up, w_mlp_down):
    bsz, seq, _ = x.shape
    depth = w_in.shape[0]
    t = bsz * seq
    tm_in = _row_tile(t, 512)
    tm_mlp = _row_tile(t, 1024)
    tf = 1024

    cos_q, sin_q = _rope_tables(seq)
    seg_q = _segment_ones(ATTN_WIDTH)
    seg_k = _segment_ones(ATTN_KV_WIDTH)
    e48 = _expand48()
    attn_bias = _band_bias()

    offs = [0]
    for s in IN_SIZES:
        offs.append(offs[-1] + s)

    x2d = x.reshape(t, D_MODEL)
    for i in range(depth):
        w_i = w_in[i].astype(BF16)
        w_parts = [w_i[:, offs[j]:offs[j + 1]] for j in range(len(IN_SIZES))]
        w_parts[2] = _pad_lanes(w_parts[2], HEAD_PAD)
        z, xbc, dt, q, k, v, glu = _in_projection(x2d, norm_mix_w[i][None, :], w_parts, tm_in)

        shp = lambda a: a.reshape(bsz, seq, a.shape[-1])
        y_ssd = _ssd_mixer(
            shp(xbc), shp(z), shp(dt), ssd_conv_w[i], ssd_conv_b[i][None, :],
            _pad_lanes(ssd_dt_bias[i][None, :], HEAD_PAD), _pad_lanes(ssd_a_log[i][None, :], HEAD_PAD),
            jnp.repeat(ssd_d[i], SSD_HEAD_DIM)[None, :], ssd_norm_w[i][None, :], e48)
        q_scale = 1.0 / math.sqrt(ATTN_HEAD_DIM)
        y_attn = _attn_mixer(
            shp(q), shp(k), shp(v), cos_q, sin_q,
            jnp.tile(q_norm_w[i] * q_scale, ATTN_Q_HEADS)[None, :],
            jnp.tile(k_norm_w[i], ATTN_KV_HEADS)[None, :],
            attn_sinks[i], seg_q, seg_k, attn_bias)
        y_cm = _cm_mixer(shp(glu), cm_dw_w[i], cm_dw_b[i][None, :], cm_ln_w[i][None, :], cm_ln_b[i][None, :])

        w_o = w_out[i].astype(BF16)
        x2d = _out_mlp(
            x2d, y_ssd.reshape(t, SSD_WIDTH), y_attn.reshape(t, ATTN_WIDTH), y_cm.reshape(t, CM_CHANNELS),
            w_o[:SSD_WIDTH], w_o[SSD_WIDTH:SSD_WIDTH + ATTN_WIDTH], w_o[SSD_WIDTH + ATTN_WIDTH:],
            norm_mlp_w[i][None, :], w_mlp_up[i].astype(BF16), w_mlp_down[i].astype(BF16), tm_mlp, tf)
    return x2d.reshape(bsz, seq, D_MODEL)
```

```python
import functools
import math

import jax
import jax.numpy as jnp
from jax import lax
from jax.experimental import pallas as pl
from jax.experimental.pallas import tpu as pltpu

F32 = jnp.float32
BF16 = jnp.bfloat16

D_MODEL = 1024
D_MIX = 2 * D_MODEL
SSD_WIDTH = D_MIX // 2
ATTN_WIDTH = D_MIX // 4
CM_CHANNELS = D_MIX // 4
SSD_HEAD_DIM = 64
SSD_HEADS = SSD_WIDTH // SSD_HEAD_DIM
SSD_STATE = 128
SSD_GROUPS = 2
SSD_CONV = 4
SSD_CHUNK = 128
SSD_BC = SSD_GROUPS * SSD_STATE
SSD_XBC = SSD_WIDTH + 2 * SSD_BC
ATTN_HEAD_DIM = 64
ATTN_Q_HEADS = ATTN_WIDTH // ATTN_HEAD_DIM
ATTN_KV_HEADS = 2
ATTN_KV_WIDTH = ATTN_KV_HEADS * ATTN_HEAD_DIM
WINDOW = 128
ROPE_THETA = 10000.0
CM_CONV_WIDTH = 31
D_FF = 4 * D_MODEL
RMS_EPS = 1e-6
LN_EPS = 1e-5
IN_SIZES = (SSD_WIDTH, SSD_XBC, SSD_HEADS, ATTN_WIDTH, ATTN_KV_WIDTH, ATTN_KV_WIDTH, 2 * CM_CHANNELS)

LANES = 128
SUBLANES = 8
HEAD_PAD = LANES
VMEM_LIMIT = 56 * 1024 * 1024
MASK_VALUE = -1e30
SSD_TAIL = SUBLANES
CM_TAIL = 32


def _silu(x):
    return x * jax.nn.sigmoid(x)


def _softplus(x):
    return jnp.maximum(x, 0.0) + jnp.log1p(jnp.exp(-jnp.abs(x)))


def _split3(x):
    hi = x.astype(BF16)
    r1 = x - hi.astype(F32)
    mid = r1.astype(BF16)
    lo = (r1 - mid.astype(F32)).astype(BF16)
    return hi, mid, lo


def _dot(a, b):
    return jnp.dot(a, b, preferred_element_type=F32)


def _dot_nt(a, b):
    return lax.dot_general(a, b, (((1,), (1,)), ((), ())), preferred_element_type=F32)


def _dot_tn(a, b):
    return lax.dot_general(a, b, (((0,), (0,)), ((), ())), preferred_element_type=F32)


def _causal_dwconv(ext_ref, base, w_ref, n_taps, tail, rows_out, acc):
    off = tail - (n_taps - 1)
    outs = []
    for l in range(ext_ref.shape[1] // LANES):
        lanes = slice(l * LANES, (l + 1) * LANES)
        acc_l = acc[:, lanes]
        for r in range(SUBLANES):
            rows = rows_out + SUBLANES if r else rows_out
            part = None
            for m in range((off + n_taps) // SUBLANES + 1):
                k = SUBLANES * m + r - off
                if 0 <= k < n_taps:
                    lo = base + SUBLANES * m
                    term = w_ref[k:k + 1, lanes] * ext_ref[pl.ds(lo, rows), lanes]
                    part = term if part is None else part + term
            if part is None:
                continue
            if r:
                part = pltpu.roll(part, rows - r, 0)
            acc_l = acc_l + part[0:rows_out]
        outs.append(acc_l)
    return jnp.concatenate(outs, axis=1)


def _inproj_kernel(tiles_per_seq, x_ref, nw_ref, wz_ref, wxbc_ref, wdt_ref, wq_ref, wkv_ref, wglu_ref,
                   cw_ref, cb_ref, z_ref, xs_ref, bc_ref, dt_ref, q_ref, kv_ref, glu_ref, ssd_ext):
    tm = x_ref.shape[0]
    q = SSD_CHUNK

    @pl.when(lax.rem(pl.program_id(0), tiles_per_seq) == 0)
    def _():
        ssd_ext[0:SSD_TAIL, :] = jnp.zeros((SSD_TAIL, SSD_XBC), F32)

    x = x_ref[...]
    ms = jnp.mean(x * x, axis=-1, keepdims=True)
    h = (x * lax.rsqrt(ms + RMS_EPS) * nw_ref[...]).astype(BF16)
    ssd_ext[SSD_TAIL:SSD_TAIL + tm, :] = _dot(h, wxbc_ref[...])
    glu_ref[...] = _dot(h, wglu_ref[...]).astype(glu_ref.dtype)
    z_ref[...] = _dot(h, wz_ref[...]).astype(z_ref.dtype)
    q_ref[...] = _dot(h, wq_ref[...]).astype(q_ref.dtype)
    kv_ref[...] = _dot(h, wkv_ref[...]).astype(kv_ref.dtype)
    dt_ref[...] = _dot(h, wdt_ref[...])

    for c in range(tm // q):
        base = c * q
        xbc = _silu(_causal_dwconv(ssd_ext, base, cw_ref, SSD_CONV, SSD_TAIL, q,
                                   jnp.broadcast_to(cb_ref[...], (q, SSD_XBC))))
        xs_ref[base:base + q, :] = xbc[:, :SSD_WIDTH].astype(xs_ref.dtype)
        bc_ref[base:base + q, :] = xbc[:, SSD_WIDTH:].astype(bc_ref.dtype)

    ssd_ext[0:SSD_TAIL, :] = ssd_ext[tm:tm + SSD_TAIL, :]


def _const_spec(shape):
    return pl.BlockSpec(shape, lambda *_: (0,) * len(shape), pipeline_mode=pl.Buffered(1))


def _in_projection(x2d, norm_w, w_parts, conv_params, tm, tiles_per_seq):
    t = x2d.shape[0]
    out_cols = [(SSD_WIDTH, BF16), (SSD_WIDTH, BF16), (2 * SSD_BC, BF16), (HEAD_PAD, F32),
                (ATTN_WIDTH, BF16), (2 * ATTN_KV_WIDTH, BF16), (2 * CM_CHANNELS, BF16)]
    row = lambda n: pl.BlockSpec((tm, n), lambda i: (i, 0))
    return pl.pallas_call(
        functools.partial(_inproj_kernel, tiles_per_seq),
        grid=(t // tm,),
        in_specs=([row(D_MODEL), _const_spec((1, D_MODEL))] + [_const_spec(w.shape) for w in w_parts]
                  + [_const_spec(p.shape) for p in conv_params]),
        out_specs=[row(n) for n, _ in out_cols],
        out_shape=[jax.ShapeDtypeStruct((t, n), d) for n, d in out_cols],
        scratch_shapes=[pltpu.VMEM((tm + SSD_TAIL, SSD_XBC), F32)],
        compiler_params=pltpu.CompilerParams(
            dimension_semantics=("arbitrary",), vmem_limit_bytes=VMEM_LIMIT),
    )(x2d, norm_w, *w_parts, *conv_params)


def _ssd_kernel(xs_ref, bc_ref, z_ref, dt_ref, dtb_ref, alog_ref, dexp_ref, nw_ref, e48_ref,
                y_ref, state_ref):
    q = SSD_CHUNK
    c = pl.program_id(1)

    @pl.when(c == 0)
    def _():
        state_ref[...] = jnp.zeros_like(state_ref)

    xs = xs_ref[0].astype(F32)
    bm = bc_ref[0, :, :SSD_BC]
    cm = bc_ref[0, :, SSD_BC:]

    dt = _softplus(dt_ref[0] + dtb_ref[...])
    a_dt = dt * (-jnp.exp(alog_ref[...]))
    row_i = lax.broadcasted_iota(jnp.int32, (q, q), 0)
    col_i = lax.broadcasted_iota(jnp.int32, (q, q), 1)
    causal = row_i >= col_i
    tril = jnp.where(causal, 1.0, 0.0).astype(BF16)
    a_cs = _dot(jnp.concatenate([tril, tril, tril], axis=1),
                jnp.concatenate(_split3(a_dt), axis=0))
    a_cs_t = a_cs.T
    a_last = a_cs[q - 1:q, :]

    lane = lax.broadcasted_iota(jnp.int32, (3 * q, HEAD_PAD), 1)
    scal = jnp.concatenate([dt, jnp.exp(a_cs), jnp.exp(a_last - a_cs)], axis=0)
    scal = jnp.where(lane < SSD_HEADS, scal, 0.0)
    hi, mid, lo = _split3(scal)
    packed = (hi.astype(F32) + pltpu.roll(mid.astype(F32), SSD_HEADS, 1)
              + pltpu.roll(lo.astype(F32), 2 * SSD_HEADS, 1)).astype(BF16)
    expanded = _dot(packed, e48_ref[...])
    dt_x = expanded[0:q]
    decay_out = expanded[q:2 * q]
    decay_end = expanded[2 * q:3 * q]

    xdt = xs * dt_x
    xdt_b = xdt.astype(BF16)
    xdt_end = (xdt * decay_end).astype(BF16)

    half = SSD_WIDTH // SSD_GROUPS
    heads_per_group = SSD_HEADS // SSD_GROUPS
    pair_lane = lax.broadcasted_iota(jnp.int32, (q, LANES), 1)
    y_blocks = []
    new_state = []
    for g in range(SSD_GROUPS):
        b_g = bm[:, g * SSD_STATE:(g + 1) * SSD_STATE]
        c_g = cm[:, g * SSD_STATE:(g + 1) * SSD_STATE]
        cb = _dot_nt(c_g, b_g)
        st_g = state_ref[:, g * half:(g + 1) * half]
        y_off = _dot(c_g, st_g.astype(BF16)) * decay_out[:, g * half:(g + 1) * half]
        for pr in range(heads_per_group // 2):
            ms_pair = []
            for hh in range(2):
                h = g * heads_per_group + 2 * pr + hh
                seg = a_cs[:, h:h + 1] - a_cs_t[h:h + 1, :]
                ms_pair.append((cb * jnp.where(causal, jnp.exp(seg), 0.0)).astype(BF16))
            lo_l = g * half + pr * LANES
            xp = xdt_b[:, lo_l:lo_l + LANES]
            rhs = jnp.concatenate([jnp.where(pair_lane < SSD_HEAD_DIM, xp, jnp.zeros_like(xp)),
                                   jnp.where(pair_lane >= SSD_HEAD_DIM, xp, jnp.zeros_like(xp))], axis=0)
            y_diag = _dot(jnp.concatenate(ms_pair, axis=1), rhs)
            y_blocks.append(y_diag + y_off[:, pr * LANES:(pr + 1) * LANES])
        new_state.append(st_g * decay_out[q - 1:q, g * half:(g + 1) * half]
                         + _dot_tn(b_g, xdt_end[:, g * half:(g + 1) * half]))
    for g in range(SSD_GROUPS):
        state_ref[:, g * half:(g + 1) * half] = new_state[g]

    y = jnp.concatenate(y_blocks, axis=1) + xs * dexp_ref[...]
    gated = y * _silu(z_ref[0].astype(F32))
    outs = []
    for g in range(SSD_GROUPS):
        gg = gated[:, g * half:(g + 1) * half]
        outs.append(gg * lax.rsqrt(jnp.mean(gg * gg, axis=-1, keepdims=True) + RMS_EPS))
    y_ref[0] = (jnp.concatenate(outs, axis=1) * nw_ref[...]).astype(y_ref.dtype)


def _ssd_mixer(xs, bc, z, dt, dt_bias, a_log, d_exp, norm_w, e48):
    bsz, seq, _ = xs.shape
    q = SSD_CHUNK
    blk = lambda n: pl.BlockSpec((1, q, n), lambda b, c: (b, c, 0))
    return pl.pallas_call(
        _ssd_kernel,
        grid=(bsz, seq // q),
        in_specs=[blk(SSD_WIDTH), blk(2 * SSD_BC), blk(SSD_WIDTH), blk(HEAD_PAD),
                  _const_spec((1, HEAD_PAD)), _const_spec((1, HEAD_PAD)),
                  _const_spec((1, SSD_WIDTH)), _const_spec((1, SSD_WIDTH)),
                  _const_spec((HEAD_PAD, SSD_WIDTH))],
        out_specs=blk(SSD_WIDTH),
        out_shape=jax.ShapeDtypeStruct((bsz, seq, SSD_WIDTH), BF16),
        scratch_shapes=[pltpu.VMEM((SSD_STATE, SSD_WIDTH), F32)],
        compiler_params=pltpu.CompilerParams(
            dimension_semantics=("arbitrary", "arbitrary"), vmem_limit_bytes=VMEM_LIMIT),
    )(xs, bc, z, dt, dt_bias, a_log, d_exp, norm_w, e48)


def _head_rms_rope(x, nw, cos, sin, seg_ones):
    hi, mid, _ = _split3(x * x)
    ms = _dot(jnp.concatenate([hi, mid], axis=1), seg_ones) * (1.0 / ATTN_HEAD_DIM)
    xn = x * lax.rsqrt(ms + RMS_EPS) * nw
    width = x.shape[1]
    lane = lax.broadcasted_iota(jnp.int32, x.shape, 1)
    hd2 = ATTN_HEAD_DIM // 2
    swapped = jnp.where((lane & (ATTN_HEAD_DIM - 1)) < hd2,
                        pltpu.roll(xn, width - hd2, 1), pltpu.roll(xn, hd2, 1))
    return xn * cos + swapped * sin


ATTN_PREP_ROWS = 256
ATTN_SLOTS = ATTN_KV_HEADS * 2


def _attn_kernel(sink_ref, q_ref, kv_ref, cos_ref, sin_ref, qnw_ref, knw_ref, segq_ref, segk_ref,
                 bias_ref, o_ref, qn_ref, kp_ref, vp_ref):
    blk = WINDOW
    hd = ATTN_HEAD_DIM
    seq = q_ref.shape[1]
    rt = ATTN_PREP_ROWS

    lane_kv = lax.broadcasted_iota(jnp.int32, (rt, ATTN_KV_WIDTH), 1)
    low = lane_kv < hd

    def place(t, g, slot):
        src = t if g == slot else pltpu.roll(t, hd, 1)
        return jnp.where(low if slot == 0 else ~low, src, 0.0).astype(BF16)

    for n in range(ATTN_SLOTS):
        kp_ref[n, 0:blk, :] = jnp.zeros((blk, ATTN_KV_WIDTH), BF16)
        vp_ref[n, 0:blk, :] = jnp.zeros((blk, 2 * ATTN_KV_WIDTH), BF16)

    def prep(t, carry):
        r0 = pl.multiple_of(t * rt, rt)
        rows = pl.ds(r0, rt)
        dst = pl.ds(r0 + blk, rt)
        cos = cos_ref[rows, :]
        sin = sin_ref[rows, :]
        qn_ref[rows, :] = _head_rms_rope(q_ref[0, rows, :].astype(F32), qnw_ref[...], cos, sin,
                                         segq_ref[...]).astype(BF16)
        kn = _head_rms_rope(kv_ref[0, rows, :ATTN_KV_WIDTH].astype(F32), knw_ref[...],
                            cos[:, :ATTN_KV_WIDTH], sin[:, :ATTN_KV_WIDTH], segk_ref[...])
        vv = kv_ref[0, rows, ATTN_KV_WIDTH:].astype(F32)
        for g in range(ATTN_KV_HEADS):
            for slot in range(2):
                kp_ref[2 * g + slot, dst, :] = place(kn, g, slot)
                ones = jnp.where(low if slot == 0 else ~low, 1.0, 0.0).astype(BF16)
                vp_ref[2 * g + slot, dst, :] = jnp.concatenate([place(vv, g, slot), ones], axis=1)
        return carry

    lax.fori_loop(0, seq // rt, prep, 0)

    lane_o = lax.broadcasted_iota(jnp.int32, (blk, LANES), 1)
    heads_per_kv = ATTN_Q_HEADS // ATTN_KV_HEADS

    def block(i, carry):
        r0 = pl.multiple_of(i * blk, blk)
        qrows = pl.ds(r0, blk)
        win = pl.ds(r0, 2 * blk)
        bias = bias_ref[jnp.minimum(i, 1)]
        for g in range(ATTN_KV_HEADS):
            v_both = jnp.concatenate([vp_ref[2 * g, win, :], vp_ref[2 * g + 1, win, :]], axis=0)
            for pr in range(heads_per_kv // 2):
                j = g * (heads_per_kv // 2) + pr
                q_pair = qn_ref[qrows, j * LANES:(j + 1) * LANES]
                ps = []
                sink_terms = []
                for hh in range(2):
                    sink = sink_ref[2 * j + hh]
                    s = _dot_nt(q_pair, kp_ref[2 * g + hh, win, :]) + bias
                    m = jnp.maximum(jnp.max(s, axis=-1, keepdims=True), sink)
                    ps.append(jnp.exp(s - m).astype(BF16))
                    sink_terms.append(jnp.exp(sink - m))
                pv = _dot(jnp.concatenate(ps, axis=1), v_both)
                denom = pv[:, LANES:] + jnp.where(lane_o < hd, sink_terms[0], sink_terms[1])
                o_ref[0, qrows, j * LANES:(j + 1) * LANES] = (pv[:, :LANES] / denom).astype(o_ref.dtype)
        return carry

    lax.fori_loop(0, seq // blk, block, 0, unroll=2)


def _attn_mixer(q, kv, cos_q, sin_q, q_nw, k_nw, sinks, seg_q, seg_k, bias):
    bsz, seq, _ = q.shape
    blk = WINDOW
    full = lambda n: pl.BlockSpec((1, seq, n), lambda b: (b, 0, 0))
    return pl.pallas_call(
        _attn_kernel,
        grid=(bsz,),
        in_specs=[pl.BlockSpec(memory_space=pltpu.SMEM),
                  full(ATTN_WIDTH), full(2 * ATTN_KV_WIDTH),
                  _const_spec((seq, ATTN_WIDTH)), _const_spec((seq, ATTN_WIDTH)),
                  _const_spec((1, ATTN_WIDTH)), _const_spec((1, ATTN_KV_WIDTH)),
                  _const_spec((2 * ATTN_WIDTH, ATTN_WIDTH)), _const_spec((2 * ATTN_KV_WIDTH, ATTN_KV_WIDTH)),
                  _const_spec((2, blk, 2 * blk))],
        out_specs=full(ATTN_WIDTH),
        out_shape=jax.ShapeDtypeStruct((bsz, seq, ATTN_WIDTH), BF16),
        scratch_shapes=[pltpu.VMEM((seq, ATTN_WIDTH), BF16),
                        pltpu.VMEM((ATTN_SLOTS, seq + blk, ATTN_KV_WIDTH), BF16),
                        pltpu.VMEM((ATTN_SLOTS, seq + blk, 2 * ATTN_KV_WIDTH), BF16)],
        compiler_params=pltpu.CompilerParams(
            dimension_semantics=("arbitrary",), vmem_limit_bytes=VMEM_LIMIT),
    )(sinks, q, kv, cos_q, sin_q, q_nw, k_nw, seg_q, seg_k, bias)


MLP_FF_CHUNK = 1024


def _conformer_chunk(glu_ref, base, cm_ext, dww_ref, dwb_ref, lnw_ref, lnb_ref):
    q = SSD_CHUNK
    u = glu_ref[pl.ds(base, q), :].astype(F32)
    cm_ext[pl.ds(CM_TAIL + base, q), :] = u[:, :CM_CHANNELS] * jax.nn.sigmoid(u[:, CM_CHANNELS:])
    acc = _causal_dwconv(cm_ext, base, dww_ref, CM_CONV_WIDTH, CM_TAIL, q,
                         jnp.broadcast_to(dwb_ref[...], (q, CM_CHANNELS)))
    mu = jnp.mean(acc, axis=-1, keepdims=True)
    cen = acc - mu
    var = jnp.mean(cen * cen, axis=-1, keepdims=True)
    return _silu(cen * lax.rsqrt(var + LN_EPS) * lnw_ref[...] + lnb_ref[...]).astype(BF16)


def _out_mlp_kernel(tiles_per_seq, x_ref, ys_ref, ya_ref, glu0_ref, glun_ref, wos_ref, woa_ref, woc_ref,
                    nw_ref, wup_ref, wdn_ref, dww_ref, dwb_ref, lnw_ref, lnb_ref, o_ref, ycm_ref, cm_ext,
                    hm_ref):
    tm = x_ref.shape[0]
    q = SSD_CHUNK
    i = pl.program_id(0)
    cm_params = (dww_ref, dwb_ref, lnw_ref, lnb_ref)

    def carry_tail():
        cm_ext[0:CM_TAIL, :] = cm_ext[tm:tm + CM_TAIL, :]

    @pl.when(i == 0)
    def _():
        cm_ext[0:CM_TAIL, :] = jnp.zeros((CM_TAIL, CM_CHANNELS), F32)
        for c in range(tm // q):
            ycm_ref[0, c * q:(c + 1) * q, :] = _conformer_chunk(glu0_ref, c * q, cm_ext, *cm_params)
        carry_tail()

    @pl.when(lax.rem(i + 1, tiles_per_seq) == 0)
    def _():
        cm_ext[0:CM_TAIL, :] = jnp.zeros((CM_TAIL, CM_CHANNELS), F32)

    slot = lax.rem(i, 2)
    x1 = (x_ref[...] + _dot(ys_ref[...], wos_ref[...]) + _dot(ya_ref[...], woa_ref[...])
          + _dot(ycm_ref[slot], woc_ref[...]))
    ms = jnp.mean(x1 * x1, axis=-1, keepdims=True)
    hm = (x1 * lax.rsqrt(ms + RMS_EPS) * nw_ref[...]).astype(BF16)
    o_ref[...] = x1
    hm_ref[...] = hm
    n_ff = D_FF // MLP_FF_CHUNK
    assert tm // q == n_ff

    def ff_chunk(f, carry):
        act = jnp.square(jnp.maximum(_dot(hm_ref[...], wup_ref[f]), 0.0)).astype(BF16)
        o_ref[...] += _dot(act, wdn_ref[pl.ds(pl.multiple_of(f * MLP_FF_CHUNK, MLP_FF_CHUNK), MLP_FF_CHUNK), :])
        base = pl.multiple_of(f * q, q)
        ycm_ref[1 - slot, pl.ds(base, q), :] = _conformer_chunk(glun_ref, base, cm_ext, *cm_params)
        return carry

    lax.fori_loop(0, n_ff, ff_chunk, 0)
    carry_tail()


def _out_mlp(x2d, y_ssd, y_attn, glu, wo_s, wo_a, wo_c, norm_w, w_up, w_down, cm_params, tm, tiles_per_seq):
    t = x2d.shape[0]
    n_tiles = t // tm
    row = lambda n: pl.BlockSpec((tm, n), lambda i: (i, 0))
    glu_first = pl.BlockSpec((tm, 2 * CM_CHANNELS), lambda i: (0, 0), pipeline_mode=pl.Buffered(1))
    glu_next = pl.BlockSpec((tm, 2 * CM_CHANNELS), lambda i: (jnp.minimum(i + 1, n_tiles - 1), 0))
    return pl.pallas_call(
        functools.partial(_out_mlp_kernel, tiles_per_seq),
        grid=(n_tiles,),
        in_specs=[row(D_MODEL), row(SSD_WIDTH), row(ATTN_WIDTH), glu_first, glu_next,
                  _const_spec(wo_s.shape), _const_spec(wo_a.shape), _const_spec(wo_c.shape),
                  _const_spec((1, D_MODEL)), _const_spec(w_up.shape), _const_spec(w_down.shape)]
                 + [_const_spec(p.shape) for p in cm_params],
        out_specs=row(D_MODEL),
        out_shape=jax.ShapeDtypeStruct((t, D_MODEL), F32),
        scratch_shapes=[pltpu.VMEM((2, tm, CM_CHANNELS), BF16),
                        pltpu.VMEM((tm + CM_TAIL, CM_CHANNELS), F32),
                        pltpu.VMEM((tm, D_MODEL), BF16)],
        compiler_params=pltpu.CompilerParams(
            dimension_semantics=("arbitrary",), vmem_limit_bytes=VMEM_LIMIT,
            ),
    )(x2d, y_ssd, y_attn, glu, glu, wo_s, wo_a, wo_c, norm_w, w_up, w_down, *cm_params)


def _row_tile(t, want):
    tm = min(t, want)
    assert t % tm == 0
    return tm


def _rope_tables(seq):
    inv_freq = ROPE_THETA ** (-jnp.arange(0, ATTN_HEAD_DIM, 2, dtype=F32) / ATTN_HEAD_DIM)
    ang = jnp.arange(seq, dtype=F32)[:, None] * inv_freq[None, :]
    cos, sin = jnp.cos(ang), jnp.sin(ang)
    cos_h = jnp.concatenate([cos, cos], axis=-1)
    sin_h = jnp.concatenate([-sin, sin], axis=-1)
    return jnp.tile(cos_h, (1, ATTN_Q_HEADS)), jnp.tile(sin_h, (1, ATTN_Q_HEADS))


def _pad_lanes(v, width):
    return jnp.pad(v, ((0, 0), (0, width - v.shape[-1])))


def _segment_ones(width):
    r = jnp.arange(2 * width) % width
    c = jnp.arange(width)
    return ((r[:, None] // ATTN_HEAD_DIM) == (c[None, :] // ATTN_HEAD_DIM)).astype(BF16)


def _band_bias():
    qi = jnp.arange(WINDOW)[:, None]
    ki = jnp.arange(2 * WINDOW)[None, :]
    diff = qi + WINDOW - ki
    band = (diff >= 0) & (diff < WINDOW)
    first = band & (ki >= WINDOW)
    return jnp.where(jnp.stack([first, band]), 0.0, MASK_VALUE).astype(F32)


def _expand48():
    r = jnp.arange(HEAD_PAD)
    c = jnp.arange(SSD_WIDTH)
    m = (r[:, None] < 3 * SSD_HEADS) & ((r[:, None] % SSD_HEADS) == (c[None, :] // SSD_HEAD_DIM))
    return m.astype(BF16)


def kernel(x, norm_mix_w, w_in, ssd_conv_w, ssd_conv_b, ssd_dt_bias, ssd_a_log, ssd_d, ssd_norm_w,
           q_norm_w, k_norm_w, attn_sinks, cm_dw_w, cm_dw_b, cm_ln_w, cm_ln_b, w_out, norm_mlp_w,
           w_mlp_up, w_mlp_down):
    bsz, seq, _ = x.shape
    depth = w_in.shape[0]
    t = bsz * seq
    tm_in = _row_tile(t, 512)
    tm_mlp = _row_tile(t, 512)

    cos_q, sin_q = _rope_tables(seq)
    seg_q = _segment_ones(ATTN_WIDTH)
    seg_k = _segment_ones(ATTN_KV_WIDTH)
    e48 = _expand48()
    attn_bias = _band_bias()

    offs = [0]
    for s in IN_SIZES:
        offs.append(offs[-1] + s)

    x2d = x.reshape(t, D_MODEL)
    for i in range(depth):
        w_i = w_in[i].astype(BF16)
        w_z, w_xbc, w_dt, w_q, w_k, w_v, w_glu = [w_i[:, offs[j]:offs[j + 1]] for j in range(len(IN_SIZES))]
        w_parts = [w_z, w_xbc, _pad_lanes(w_dt, HEAD_PAD), w_q, jnp.concatenate([w_k, w_v], axis=1), w_glu]
        conv_params = [ssd_conv_w[i], ssd_conv_b[i][None, :]]
        cm_params = [cm_dw_w[i], cm_dw_b[i][None, :], cm_ln_w[i][None, :], cm_ln_b[i][None, :]]
        z, xs, bc, dt, q, kv, glu = _in_projection(
            x2d, norm_mix_w[i][None, :], w_parts, conv_params, tm_in, seq // tm_in)

        shp = lambda a: a.reshape(bsz, seq, a.shape[-1])
        y_ssd = _ssd_mixer(
            shp(xs), shp(bc), shp(z), shp(dt),
            _pad_lanes(ssd_dt_bias[i][None, :], HEAD_PAD), _pad_lanes(ssd_a_log[i][None, :], HEAD_PAD),
            jnp.repeat(ssd_d[i], SSD_HEAD_DIM)[None, :], ssd_norm_w[i][None, :], e48)
        q_scale = 1.0 / math.sqrt(ATTN_HEAD_DIM)
        y_attn = _attn_mixer(
            shp(q), shp(kv), cos_q, sin_q,
            jnp.tile(q_norm_w[i] * q_scale, ATTN_Q_HEADS)[None, :],
            jnp.tile(k_norm_w[i], ATTN_KV_HEADS)[None, :],
            attn_sinks[i], seg_q, seg_k, attn_bias)

        w_o = w_out[i].astype(BF16)
        x2d = _out_mlp(
            x2d, y_ssd.reshape(t, SSD_WIDTH), y_attn.reshape(t, ATTN_WIDTH), glu,
            w_o[:SSD_WIDTH], w_o[SSD_WIDTH:SSD_WIDTH + ATTN_WIDTH], w_o[SSD_WIDTH + ATTN_WIDTH:],
            norm_mlp_w[i][None, :],
            w_mlp_up[i].astype(BF16).reshape(D_MODEL, D_FF // MLP_FF_CHUNK, MLP_FF_CHUNK).transpose(1, 0, 2),
            w_mlp_down[i].astype(BF16), cm_params,
            tm_mlp, seq // tm_mlp)
    return x2d.reshape(bsz, seq, D_MODEL)
```
